```python
import math
import jax, jax.numpy as jnp
from jax import lax
import numpy as np

D_MODEL = 1024
BATCH = 8
SEQ = 2048
DEPTH = 1
DEC_BATCH = 128
DEC_SEQ = 8
PAST_LEN = 16384
PAGE_SIZE = 128

D_FF = 2816
MIX_WIDTH = D_MODEL
GLA_WIDTH = MIX_WIDTH // 2
CONV_CH = MIX_WIDTH - GLA_WIDTH
GLA_HEADS = 4
GLA_DK = GLA_WIDTH // (2 * GLA_HEADS)
GLA_DV = GLA_WIDTH // GLA_HEADS
GLA_QK = GLA_HEADS * GLA_DK
GATE_RANK = 16
GATE_NORMALIZER = 16.0
GLA_CHUNK = 64
CONV_K = 31
IN_DIM = 2 * GLA_QK + 2 * GLA_WIDTH + GATE_RANK + 2 * CONV_CH
SPLITS = [GLA_QK, 2 * GLA_QK, 2 * GLA_QK + GLA_WIDTH, 2 * GLA_QK + 2 * GLA_WIDTH,
          2 * GLA_QK + 2 * GLA_WIDTH + GATE_RANK,
          2 * GLA_QK + 2 * GLA_WIDTH + GATE_RANK + CONV_CH]
EPS = 1e-6

kernel_name = "gla_conformer_conv_hybrid_step"


def rmsnorm(x, w):
    xf = x.astype(jnp.float32)
    y = xf * lax.rsqrt(jnp.mean(xf * xf, axis=-1, keepdims=True) + EPS)
    return (y * w.astype(jnp.float32)).astype(x.dtype)


def layernorm(x, w, b):
    xf = x.astype(jnp.float32)
    mu = jnp.mean(xf, axis=-1, keepdims=True)
    var = jnp.mean(jnp.square(xf - mu), axis=-1, keepdims=True)
    y = (xf - mu) * lax.rsqrt(var + EPS)
    return (y * w.astype(jnp.float32) + b.astype(jnp.float32)).astype(x.dtype)


def swiglu(x, wg, wu, wd):
    return (jax.nn.silu(x @ wg) * (x @ wu)) @ wd


def gla_recurrence(q, k, v, logg, s0):
    B, H, L, DK = q.shape
    DV = v.shape[-1]
    c = math.gcd(L, GLA_CHUNK)
    n = L // c

    def to_chunks(t):
        return t.reshape(B, H, n, c, t.shape[-1]).transpose(2, 0, 1, 3, 4)

    qc, kc, vc, gc = to_chunks(q), to_chunks(k), to_chunks(v), to_chunks(logg)
    causal = jnp.tril(jnp.ones((c, c), dtype=bool))[:, :, None]

    def step(s, inp):
        qi, ki, vi, gi = inp
        b = jnp.cumsum(gi, axis=2)
        diff = b[:, :, :, None, :] - b[:, :, None, :, :]
        decay = jnp.exp(jnp.where(causal, diff, -jnp.inf))
        scores = jnp.einsum('bhid,bhijd,bhjd->bhij', qi, decay, ki)
        o = (jnp.einsum('bhij,bhjv->bhiv', scores, vi)
             + jnp.einsum('bhid,bhdv->bhiv', qi * jnp.exp(b), s))
        b_last = b[:, :, -1:, :]
        s_new = (jnp.exp(b_last[:, :, 0, :])[..., None] * s
                 + jnp.einsum('bhjd,bhjv->bhdv', ki * jnp.exp(b_last - b), vi))
        return s_new, o

    s_fin, oc = lax.scan(step, s0, (qc, kc, vc, gc))
    o = oc.transpose(1, 2, 0, 3, 4).reshape(B, H, L, DV)
    return o, s_fin


def hybrid_layer(x, s_gla, conv_buf,
                 ffn1_norm, ffn1_wg, ffn1_wu, ffn1_wd,
                 mix_norm, w_in, w_a2, b_a, gla_onorm,
                 w_dw, b_dw, conv_ln_w, conv_ln_b, w_out,
                 ffn2_norm, ffn2_wg, ffn2_wu, ffn2_wd):
    B, L, _ = x.shape
    x = x + 0.5 * swiglu(rmsnorm(x, ffn1_norm), ffn1_wg, ffn1_wu, ffn1_wd)
    h = rmsnorm(x, mix_norm)
    z = h @ w_in
    q, k, v, g, a_lr, u_a, u_b = jnp.split(z, SPLITS, axis=-1)

    def heads(t, d):
        return t.reshape(B, L, GLA_HEADS, d).transpose(0, 2, 1, 3).astype(jnp.float32)
    logg = jax.nn.log_sigmoid((a_lr @ w_a2 + b_a).astype(jnp.float32)) / GATE_NORMALIZER
    qh = heads(q, GLA_DK) * (GLA_DK ** -0.5)
    o, s_new = gla_recurrence(qh, heads(k, GLA_DK), heads(v, GLA_DV), heads(logg, GLA_DK),
                              s_gla.astype(jnp.float32))
    o = o * lax.rsqrt(jnp.mean(o * o, axis=-1, keepdims=True) + EPS) * gla_onorm.astype(jnp.float32)
    o = o.transpose(0, 2, 1, 3).reshape(B, L, GLA_WIDTH).astype(x.dtype)
    o_gla = o * jax.nn.silu(g)

    u = u_a * jax.nn.sigmoid(u_b)
    full = jnp.concatenate([conv_buf.astype(u.dtype), u], axis=1)
    yc = lax.conv_general_dilated(full, w_dw[:, None, :].astype(u.dtype), (1,), 'VALID',
                                  dimension_numbers=('NWC', 'WIO', 'NWC'),
                                  feature_group_count=CONV_CH) + b_dw
    new_buf = full[:, full.shape[1] - (CONV_K - 1):, :]
    o_conv = jax.nn.silu(layernorm(yc, conv_ln_w, conv_ln_b))

    x = x + jnp.concatenate([o_gla, o_conv], axis=-1) @ w_out
    x = x + 0.5 * swiglu(rmsnorm(x, ffn2_norm), ffn2_wg, ffn2_wu, ffn2_wd)
    return x, s_new, new_buf


def setup_inputs(seed: int = 0) -> dict:
    key = jax.random.key(seed)
    ks = iter(jax.random.split(key, 40))
    f32 = jnp.float32

    def nrm(shape, scale):
        return jax.random.normal(next(ks), shape, f32) * scale

    def gain(shape):
        return 1.0 + nrm(shape, 0.02)

    return {
        "x_prompt": nrm((BATCH, SEQ, D_MODEL), 1.0),
        "x_sample": nrm((DEC_BATCH, DEC_SEQ, D_MODEL), 1.0),
        "state_gla": nrm((DEPTH, DEC_BATCH, GLA_HEADS, GLA_DK, GLA_DV), 0.5),
        "state_conv": nrm((DEPTH, DEC_BATCH, CONV_K - 1, CONV_CH), 0.5),
        "ffn1_norm": gain((DEPTH, D_MODEL)),
        "ffn1_wg": nrm((DEPTH, D_MODEL, D_FF), D_MODEL ** -0.5),
        "ffn1_wu": nrm((DEPTH, D_MODEL, D_FF), D_MODEL ** -0.5),
        "ffn1_wd": nrm((DEPTH, D_FF, D_MODEL), D_FF ** -0.5),
        "mix_norm": gain((DEPTH, D_MODEL)),
        "w_in": nrm((DEPTH, D_MODEL, IN_DIM), D_MODEL ** -0.5),
        "w_a2": nrm((DEPTH, GATE_RANK, GLA_QK), GATE_RANK ** -0.5),
        "b_a": nrm((DEPTH, GLA_QK), 0.1),
        "gla_onorm": gain((DEPTH, GLA_DV)),
        "w_dw": nrm((DEPTH, CONV_K, CONV_CH), CONV_K ** -0.5),
        "b_dw": nrm((DEPTH, CONV_CH), 0.02),
        "conv_ln_w": gain((DEPTH, CONV_CH)),
        "conv_ln_b": nrm((DEPTH, CONV_CH), 0.02),
        "w_out": nrm((DEPTH, MIX_WIDTH, D_MODEL), MIX_WIDTH ** -0.5),
        "ffn2_norm": gain((DEPTH, D_MODEL)),
        "ffn2_wg": nrm((DEPTH, D_MODEL, D_FF), D_MODEL ** -0.5),
        "ffn2_wu": nrm((DEPTH, D_MODEL, D_FF), D_MODEL ** -0.5),
        "ffn2_wd": nrm((DEPTH, D_FF, D_MODEL), D_FF ** -0.5),
        "final_norm": gain((D_MODEL,)),
    }


def reference(x_prompt, x_sample, state_gla, state_conv,
              ffn1_norm, ffn1_wg, ffn1_wu, ffn1_wd,
              mix_norm, w_in, w_a2, b_a, gla_onorm,
              w_dw, b_dw, conv_ln_w, conv_ln_b, w_out,
              ffn2_norm, ffn2_wg, ffn2_wu, ffn2_wd, final_norm):
    xp, xs = x_prompt, x_sample
    gla_p, conv_p, gla_s, conv_s = [], [], [], []
    for l in range(DEPTH):
        w = (ffn1_norm[l], ffn1_wg[l], ffn1_wu[l], ffn1_wd[l],
             mix_norm[l], w_in[l], w_a2[l], b_a[l], gla_onorm[l],
             w_dw[l], b_dw[l], conv_ln_w[l], conv_ln_b[l], w_out[l],
             ffn2_norm[l], ffn2_wg[l], ffn2_wu[l], ffn2_wd[l])
        s0 = jnp.zeros((xp.shape[0], GLA_HEADS, GLA_DK, GLA_DV), jnp.float32)
        b0 = jnp.zeros((xp.shape[0], CONV_K - 1, CONV_CH), xp.dtype)
        xp, sp, bp = hybrid_layer(xp, s0, b0, *w)
        xs, ss, bs = hybrid_layer(xs, state_gla[l], state_conv[l], *w)
        gla_p.append(sp.astype(state_gla.dtype)); conv_p.append(bp.astype(state_conv.dtype))
        gla_s.append(ss.astype(state_gla.dtype)); conv_s.append(bs.astype(state_conv.dtype))
    y_prompt = rmsnorm(xp, final_norm)
    y_sample = rmsnorm(xs, final_norm)
    new_gla_prompt = jnp.stack(gla_p)
    new_conv_prompt = jnp.stack(conv_p)
    new_gla_sample = jnp.stack(gla_s)
    new_conv_sample = jnp.stack(conv_s)
    return (y_prompt, y_sample, new_gla_prompt, new_conv_prompt, new_gla_sample, new_conv_sample)
```

```python
import functools

import numpy as np
import jax
import jax.numpy as jnp
from jax import lax
from jax.experimental import pallas as pl
from jax.experimental.pallas import tpu as pltpu

F32 = jnp.float32
BF16 = jnp.bfloat16
HIGHEST = lax.Precision.HIGHEST

EPS = 1e-6
GLA_HEADS = 4
GLA_DK = 64
GLA_DV = 128
GLA_QK = GLA_HEADS * GLA_DK
GLA_WIDTH = GLA_HEADS * GLA_DV
GATE_NORMALIZER = 16.0
GLA_CHUNK = 64
CONV_K = 31
CONV_HIST = CONV_K - 1
CONV_PAD = 32

TOKEN_TILE = 256
SEQ_TILE = 256
SAMPLE_SEQS = 32
VMEM_LIMIT_BYTES = 56 * 1024 * 1024


def _rms(x, w):
    return x * lax.rsqrt(jnp.mean(x * x, axis=-1, keepdims=True) + EPS) * w


def _silu(x):
    return x * jax.nn.sigmoid(x)


def _swiglu(hb, wg_ref, wu_ref, wd_ref):
    g = jnp.dot(hb, wg_ref[...], preferred_element_type=F32)
    u = jnp.dot(hb, wu_ref[...], preferred_element_type=F32)
    a = (_silu(g) * u).astype(BF16)
    return jnp.dot(a, wd_ref[...], preferred_element_type=F32)


def _log_sigmoid(x):
    return jnp.minimum(x, 0.0) - jnp.log1p(jnp.exp(-jnp.abs(x)))


def _ffn_in_kernel(x_ref, n1_ref, wg_ref, wu_ref, wd_ref, nm_ref, win_ref, wa2_ref, ba_ref,
                   x1_ref, q_ref, k_ref, v_ref, sg_ref, lg_ref, u_ref):
    x = x_ref[...]
    h = _rms(x, n1_ref[...]).astype(BF16)
    x1 = x + 0.5 * _swiglu(h, wg_ref, wu_ref, wd_ref)
    x1_ref[...] = x1
    h2 = _rms(x1, nm_ref[...]).astype(BF16)
    z = jnp.dot(h2, win_ref[...], preferred_element_type=F32)
    o = 0
    q_ref[...] = z[:, o:o + GLA_QK] * (GLA_DK ** -0.5)
    o += GLA_QK
    k_ref[...] = z[:, o:o + GLA_QK]
    o += GLA_QK
    v_ref[...] = z[:, o:o + GLA_WIDTH]
    o += GLA_WIDTH
    sg_ref[...] = _silu(z[:, o:o + GLA_WIDTH])
    o += GLA_WIDTH
    cc = u_ref.shape[-1]
    u_ref[...] = z[:, o:o + cc] * jax.nn.sigmoid(z[:, o + cc:o + 2 * cc])
    o += 2 * cc
    alr = z[:, o:].astype(BF16)
    xg = jnp.dot(alr, wa2_ref[...], preferred_element_type=F32) + ba_ref[...]
    lg_ref[...] = _log_sigmoid(xg) * (1.0 / GATE_NORMALIZER)


def _const_spec(shape):
    nd = len(shape)
    return pl.BlockSpec(shape, lambda *_: (0,) * nd, pipeline_mode=pl.Buffered(1))


def _ffn_in(x, n1, wg, wu, wd, nm, win, wa2, ba, conv_ch):
    m, d = x.shape
    tm = TOKEN_TILE
    assert m % tm == 0
    row = lambda w: pl.BlockSpec((tm, w), lambda i: (i, 0))
    widths = (d, GLA_QK, GLA_QK, GLA_WIDTH, GLA_WIDTH, GLA_QK, conv_ch)
    return pl.pallas_call(
        _ffn_in_kernel,
        grid=(m // tm,),
        in_specs=[row(d)] + [_const_spec(a.shape) for a in (n1, wg, wu, wd, nm, win, wa2, ba)],
        out_specs=[row(w) for w in widths],
        out_shape=[jax.ShapeDtypeStruct((m, w), F32) for w in widths],
        compiler_params=pltpu.CompilerParams(dimension_semantics=("arbitrary",), vmem_limit_bytes=VMEM_LIMIT_BYTES),
        name="ffn_in",
    )(x, n1, wg, wu, wd, nm, win, wa2, ba)


def _ffn_out_kernel(x1_ref, mix_ref, wout_ref, n2_ref, wg_ref, wu_ref, wd_ref, nf_ref, y_ref, *, final):
    x2 = x1_ref[...] + jnp.dot(mix_ref[...], wout_ref[...], preferred_element_type=F32)
    h = _rms(x2, n2_ref[...]).astype(BF16)
    x3 = x2 + 0.5 * _swiglu(h, wg_ref, wu_ref, wd_ref)
    y_ref[...] = _rms(x3, nf_ref[...]) if final else x3


def _ffn_out(x1, mix, wout, n2, wg, wu, wd, nf, final):
    m, d = x1.shape
    tm = TOKEN_TILE
    assert m % tm == 0
    row = lambda w: pl.BlockSpec((tm, w), lambda i: (i, 0))
    return pl.pallas_call(
        functools.partial(_ffn_out_kernel, final=final),
        grid=(m // tm,),
        in_specs=[row(d), row(mix.shape[1])] + [_const_spec(a.shape) for a in (wout, n2, wg, wu, wd, nf)],
        out_specs=row(d),
        out_shape=jax.ShapeDtypeStruct((m, d), F32),
        compiler_params=pltpu.CompilerParams(dimension_semantics=("arbitrary",), vmem_limit_bytes=VMEM_LIMIT_BYTES),
        name="ffn_out",
    )(x1, mix, wout, n2, wg, wu, wd, nf)


def _gla_decays(q, k, lg, ltri, chunk):
    t = q.shape[0]
    b = jnp.dot(ltri, lg, precision=HIGHEST, preferred_element_type=F32)
    b3 = b.reshape(t // chunk, chunk, b.shape[-1])
    tot = jnp.broadcast_to(b3[:, chunk - 1:chunk, :], b3.shape).reshape(b.shape)
    return q * jnp.exp(b), k * jnp.exp(-b), k * jnp.exp(tot - b), jnp.exp(tot)


def _head_rms(o, w):
    return o * lax.rsqrt(jnp.mean(o * o, axis=-1, keepdims=True) + EPS) * w


def _conv_post(yc, lnw, lnb):
    mu = jnp.mean(yc, axis=-1, keepdims=True)
    d = yc - mu
    var = jnp.mean(d * d, axis=-1, keepdims=True)
    return _silu(d * lax.rsqrt(var + EPS) * lnw + lnb)


def _mix_prompt_kernel(q_ref, k_ref, v_ref, sg_ref, lg_ref, u_ref, on_ref, wdw_ref, bdw_ref, lnw_ref, lnb_ref,
                       cmask_ref, mix_ref, ns_ref, nc_ref, s_scr, ubuf):
    t = pl.program_id(1)
    tt = q_ref.shape[0]
    n_chunks = tt // GLA_CHUNK

    @pl.when(t == 0)
    def _():
        s_scr[...] = jnp.zeros_like(s_scr)
        ubuf[0:CONV_PAD, :] = jnp.zeros((CONV_PAD, ubuf.shape[1]), F32)

    cmask = cmask_ref[...]
    qd, kd, kh, dec = _gla_decays(q_ref[...], k_ref[...], lg_ref[...], cmask, GLA_CHUNK)
    kh_t = kh.T
    dec_t = dec.T
    vb = v_ref[...].astype(BF16)
    causal = cmask > 0.5
    onorm = on_ref[...]
    for h in range(GLA_HEADS):
        dk = slice(h * GLA_DK, (h + 1) * GLA_DK)
        dv = slice(h * GLA_DV, (h + 1) * GLA_DV)
        qh = qd[:, dk].astype(BF16)
        a = lax.dot_general(qh, kd[:, dk].astype(BF16), (((1,), (1,)), ((), ())), preferred_element_type=F32)
        a = jnp.where(causal, a, 0.0).astype(BF16)
        vh = vb[:, dv]
        o_intra = jnp.dot(a, vh, preferred_element_type=F32)
        s = s_scr[h]
        parts = []
        for c in range(n_chunks):
            r = slice(c * GLA_CHUNK, (c + 1) * GLA_CHUNK)
            parts.append(o_intra[r] + jnp.dot(qh[r], s.astype(BF16), preferred_element_type=F32))
            ktv = jnp.dot(kh_t[dk, r].astype(BF16), vh[r], preferred_element_type=F32)
            s = s * dec_t[dk, c * GLA_CHUNK:c * GLA_CHUNK + 1] + ktv
        s_scr[h] = s
        o = _head_rms(jnp.concatenate(parts, axis=0), onorm)
        mix_ref[:, dv] = (o * sg_ref[:, dv]).astype(BF16)

    cc = u_ref.shape[1]
    ubuf[CONV_PAD:CONV_PAD + tt, :] = u_ref[...]
    acc = jnp.zeros((tt, cc), F32)
    for j in range(CONV_K):
        acc = acc + ubuf[pl.ds(CONV_PAD - CONV_HIST + j, tt), :] * wdw_ref[j:j + 1, :]
    oc = _conv_post(acc + bdw_ref[...], lnw_ref[...], lnb_ref[...])
    mix_ref[:, GLA_WIDTH:GLA_WIDTH + cc] = oc.astype(BF16)
    ubuf[0:CONV_PAD, :] = ubuf[tt:tt + CONV_PAD, :]

    @pl.when(t == pl.num_programs(1) - 1)
    def _():
        ns_ref[0] = s_scr[...]
        nc_ref[0] = u_ref[tt - CONV_HIST:tt, :]


def _chunk_masks(t, chunk):
    i = np.arange(t)
    same = (i[:, None] // chunk) == (i[None, :] // chunk)
    causal = same & (i[:, None] >= i[None, :])
    return jnp.asarray(causal, F32)


def _mix_prompt(q, k, v, sg, lg, u, onorm, wdw, bdw, lnw, lnb, batch, seq):
    tt = SEQ_TILE
    assert seq % tt == 0 and tt % GLA_CHUNK == 0 and tt >= CONV_PAD
    nt = seq // tt
    cc = u.shape[1]
    causal = _chunk_masks(tt, GLA_CHUNK)
    row = lambda w: pl.BlockSpec((tt, w), lambda b, t: (b * nt + t, 0))
    consts = (onorm, wdw, bdw, lnw, lnb, causal)
    return pl.pallas_call(
        _mix_prompt_kernel,
        grid=(batch, nt),
        in_specs=[row(a.shape[1]) for a in (q, k, v, sg, lg, u)] + [_const_spec(a.shape) for a in consts],
        out_specs=[row(GLA_WIDTH + cc),
                   pl.BlockSpec((1, GLA_HEADS, GLA_DK, GLA_DV), lambda b, t: (b, 0, 0, 0)),
                   pl.BlockSpec((1, CONV_HIST, cc), lambda b, t: (b, 0, 0))],
        out_shape=[jax.ShapeDtypeStruct((batch * seq, GLA_WIDTH + cc), BF16),
                   jax.ShapeDtypeStruct((batch, GLA_HEADS, GLA_DK, GLA_DV), F32),
                   jax.ShapeDtypeStruct((batch, CONV_HIST, cc), F32)],
        scratch_shapes=[pltpu.VMEM((GLA_HEADS, GLA_DK, GLA_DV), F32), pltpu.VMEM((CONV_PAD + tt, cc), F32)],
        compiler_params=pltpu.CompilerParams(dimension_semantics=("arbitrary", "arbitrary"),
                                             vmem_limit_bytes=VMEM_LIMIT_BYTES),
        name="mix_prompt",
    )(q, k, v, sg, lg, u, *consts)


def _mix_sample_kernel(q_ref, k_ref, v_ref, sg_ref, lg_ref, u_ref, s0_ref, c0_ref, on_ref, wdw_ref, bdw_ref,
                       lnw_ref, lnb_ref, cmask_ref, mix_ref, ns_ref, nc_ref, full, *, dec_seq):
    tt = q_ref.shape[0]
    bs = tt // dec_seq
    cmask = cmask_ref[...]
    qd, kd, kh, dec = _gla_decays(q_ref[...], k_ref[...], lg_ref[...], cmask, dec_seq)
    dec_t = dec.T
    v = v_ref[...]
    causal = cmask > 0.5
    onorm = on_ref[...]
    per_seq = lambda a: a.reshape(bs, dec_seq, a.shape[-1]).astype(BF16)
    for h in range(GLA_HEADS):
        dk = slice(h * GLA_DK, (h + 1) * GLA_DK)
        dv = slice(h * GLA_DV, (h + 1) * GLA_DV)
        qh = qd[:, dk]
        a = lax.dot_general(qh.astype(BF16), kd[:, dk].astype(BF16), (((1,), (1,)), ((), ())),
                            preferred_element_type=F32)
        a = jnp.where(causal, a, 0.0).astype(BF16)
        vh = v[:, dv]
        o = jnp.dot(a, vh.astype(BF16), preferred_element_type=F32)
        s0 = s0_ref[:, h]
        o = o + jnp.einsum('sqd,sdv->sqv', per_seq(qh), s0.astype(BF16),
                           preferred_element_type=F32).reshape(tt, GLA_DV)
        ktv = jnp.einsum('sqd,sqv->sdv', per_seq(kh[:, dk]), per_seq(vh), preferred_element_type=F32)
        for s in range(bs):
            ns_ref[s, h] = s0[s] * dec_t[dk, s * dec_seq:s * dec_seq + 1] + ktv[s]
        o = _head_rms(o, onorm)
        mix_ref[:, dv] = (o * sg_ref[:, dv]).astype(BF16)

    cc = u_ref.shape[1]
    full[:, 0:CONV_HIST, :] = c0_ref[...]
    full[:, CONV_HIST:CONV_HIST + dec_seq, :] = u_ref[...].reshape(bs, dec_seq, cc)
    acc = jnp.zeros((bs, dec_seq, cc), F32)
    for j in range(CONV_K):
        acc = acc + full[:, pl.ds(j, dec_seq), :] * wdw_ref[j:j + 1, :]
    oc = _conv_post(acc.reshape(tt, cc) + bdw_ref[...], lnw_ref[...], lnb_ref[...])
    mix_ref[:, GLA_WIDTH:GLA_WIDTH + cc] = oc.astype(BF16)
    nc_ref[...] = full[:, dec_seq:dec_seq + CONV_HIST, :]


def _mix_sample(q, k, v, sg, lg, u, s0, c0, onorm, wdw, bdw, lnw, lnb, dec_batch, dec_seq):
    bs = SAMPLE_SEQS
    assert dec_batch % bs == 0 and dec_seq % 8 == 0
    tt = bs * dec_seq
    cc = u.shape[1]
    causal = _chunk_masks(tt, dec_seq)
    row = lambda w: pl.BlockSpec((tt, w), lambda i: (i, 0))
    st_spec = pl.BlockSpec((bs, GLA_HEADS, GLA_DK, GLA_DV), lambda i: (i, 0, 0, 0))
    cv_spec = pl.BlockSpec((bs, CONV_HIST, cc), lambda i: (i, 0, 0))
    consts = (onorm, wdw, bdw, lnw, lnb, causal)
    full_rows = -(-(CONV_HIST + dec_seq) // 8) * 8
    return pl.pallas_call(
        functools.partial(_mix_sample_kernel, dec_seq=dec_seq),
        grid=(dec_batch // bs,),
        in_specs=[row(a.shape[1]) for a in (q, k, v, sg, lg, u)] + [st_spec, cv_spec]
        + [_const_spec(a.shape) for a in consts],
        out_specs=[row(GLA_WIDTH + cc), st_spec, cv_spec],
        out_shape=[jax.ShapeDtypeStruct((dec_batch * dec_seq, GLA_WIDTH + cc), BF16),
                   jax.ShapeDtypeStruct(s0.shape, F32),
                   jax.ShapeDtypeStruct(c0.shape, F32)],
        scratch_shapes=[pltpu.VMEM((bs, full_rows, cc), F32)],
        compiler_params=pltpu.CompilerParams(dimension_semantics=("arbitrary",), vmem_limit_bytes=VMEM_LIMIT_BYTES),
        name="mix_sample",
    )(q, k, v, sg, lg, u, s0, c0, *consts)


def kernel(x_prompt, x_sample, state_gla, state_conv, ffn1_norm, ffn1_wg, ffn1_wu, ffn1_wd, mix_norm, w_in, w_a2, b_a,
           gla_onorm, w_dw, b_dw, conv_ln_w, conv_ln_b, w_out, ffn2_norm, ffn2_wg, ffn2_wu, ffn2_wd, final_norm):
    depth = ffn1_norm.shape[0]
    batch, seq, d = x_prompt.shape
    dec_batch, dec_seq, _ = x_sample.shape
    cc = w_dw.shape[-1]
    rank = w_a2.shape[1]
    row = lambda a: a.reshape(1, -1)

    xp = x_prompt.reshape(batch * seq, d)
    xs = x_sample.reshape(dec_batch * dec_seq, d)
    gla_p, conv_p, gla_s, conv_s = [], [], [], []
    for l in range(depth):
        o_alr = 2 * GLA_QK + 2 * GLA_WIDTH
        win = jnp.concatenate([w_in[l][:, :o_alr], w_in[l][:, o_alr + rank:], w_in[l][:, o_alr:o_alr + rank]],
                              axis=1).astype(BF16)
        in_w = (row(ffn1_norm[l]), ffn1_wg[l].astype(BF16), ffn1_wu[l].astype(BF16), ffn1_wd[l].astype(BF16),
                row(mix_norm[l]), win, w_a2[l].astype(BF16), row(b_a[l]))
        mix_w = (row(gla_onorm[l]), w_dw[l], row(b_dw[l]), row(conv_ln_w[l]), row(conv_ln_b[l]))
        out_w = (w_out[l].astype(BF16), row(ffn2_norm[l]), ffn2_wg[l].astype(BF16), ffn2_wu[l].astype(BF16),
                 ffn2_wd[l].astype(BF16))
        last = l == depth - 1
        nf = row(final_norm)

        xp1, *zp = _ffn_in(xp, *in_w, cc)
        mixp, sp, bp = _mix_prompt(*zp, *mix_w, batch, seq)
        xs1, *zs = _ffn_in(xs, *in_w, cc)
        mixs, ss, bs = _mix_sample(*zs, state_gla[l], state_conv[l], *mix_w, dec_batch, dec_seq)
        xp = _ffn_out(xp1, mixp, *out_w, nf, last)
        xs = _ffn_out(xs1, mixs, *out_w, nf, last)
        gla_p.append(sp)
        conv_p.append(bp)
        gla_s.append(ss)
        conv_s.append(bs)
    return (xp.reshape(batch, seq, d), xs.reshape(dec_batch, dec_seq, d),
            jnp.stack(gla_p), jnp.stack(conv_p), jnp.stack(gla_s), jnp.stack(conv_s))
```

```python
import functools

import numpy as np
import jax
import jax.numpy as jnp
from jax import lax
from jax.experimental import pallas as pl
from jax.experimental.pallas import tpu as pltpu

F32 = jnp.float32
BF16 = jnp.bfloat16

EPS = 1e-6
GLA_HEADS = 4
GLA_DK = 64
GLA_DV = 128
GLA_QK = GLA_HEADS * GLA_DK
GLA_WIDTH = GLA_HEADS * GLA_DV
GATE_NORMALIZER = 16.0
GLA_CHUNK = 64
CONV_K = 31
CONV_HIST = CONV_K - 1
SUBLANES = 8
CONV_PAD = 32
CONV_ROWS = 32

TOKEN_TILE = 256
SAMPLE_SEQS = 32
VMEM_LIMIT_BYTES = 56 * 1024 * 1024


def _rms(x, w):
    return x * lax.rsqrt(jnp.mean(x * x, axis=-1, keepdims=True) + EPS) * w


def _silu(x):
    return x * jax.nn.sigmoid(x)


def _swiglu(hb, wg_ref, wu_ref, wd_ref):
    g = jnp.dot(hb, wg_ref[...], preferred_element_type=F32)
    u = jnp.dot(hb, wu_ref[...], preferred_element_type=F32)
    a = (_silu(g) * u).astype(BF16)
    return jnp.dot(a, wd_ref[...], preferred_element_type=F32)


def _log_sigmoid(x):
    return jnp.minimum(x, 0.0) - jnp.log1p(jnp.exp(-jnp.abs(x)))


def _ffn_in_math(x, n1_ref, wg_ref, wu_ref, wd_ref, nm_ref, win_ref, wa2_ref, ba_ref, conv_ch):
    h = _rms(x, n1_ref[...]).astype(BF16)
    x1 = x + 0.5 * _swiglu(h, wg_ref, wu_ref, wd_ref)
    h2 = _rms(x1, nm_ref[...]).astype(BF16)
    z = jnp.dot(h2, win_ref[...], preferred_element_type=F32)
    o = 0
    q = z[:, o:o + GLA_QK] * (GLA_DK ** -0.5)
    o += GLA_QK
    k = z[:, o:o + GLA_QK]
    o += GLA_QK
    v = z[:, o:o + GLA_WIDTH]
    o += GLA_WIDTH
    sg = _silu(z[:, o:o + GLA_WIDTH])
    o += GLA_WIDTH
    u = z[:, o:o + conv_ch] * jax.nn.sigmoid(z[:, o + conv_ch:o + 2 * conv_ch])
    o += 2 * conv_ch
    xg = jnp.dot(z[:, o:].astype(BF16), wa2_ref[...], preferred_element_type=F32) + ba_ref[...]
    lg = _log_sigmoid(xg) * (1.0 / GATE_NORMALIZER)
    return x1, q, k, v, sg, lg, u


def _const_spec(shape):
    nd = len(shape)
    return pl.BlockSpec(shape, lambda *_: (0,) * nd, pipeline_mode=pl.Buffered(1))


def _gla_decays(q, k, lg, cmask, chunk):
    t = q.shape[0]
    cm = cmask.astype(BF16)
    hi = lg.astype(BF16)
    lo = (lg - hi.astype(F32)).astype(BF16)
    b = jnp.dot(cm, hi, preferred_element_type=F32) + jnp.dot(cm, lo, preferred_element_type=F32)
    b3 = b.reshape(t // chunk, chunk, b.shape[-1])
    tot = jnp.broadcast_to(b3[:, chunk - 1:chunk, :], b3.shape).reshape(b.shape)
    dec = jnp.exp(tot)
    kd = k * jnp.exp(-b)
    return q * jnp.exp(b), kd, kd * dec, dec


def _head_rms(o, w):
    return o * lax.rsqrt(jnp.mean(o * o, axis=-1, keepdims=True) + EPS) * w


def _conv_post(yc, lnw, lnb):
    mu = jnp.mean(yc, axis=-1, keepdims=True)
    d = yc - mu
    var = jnp.mean(d * d, axis=-1, keepdims=True)
    return _silu(d * lax.rsqrt(var + EPS) * lnw + lnb)


def _chunk_masks(t, chunk):
    i = np.arange(t)
    same = (i[:, None] // chunk) == (i[None, :] // chunk)
    causal = same & (i[:, None] >= i[None, :])
    return jnp.asarray(causal, F32)


def _conv_tap_groups():
    base = CONV_PAD - CONV_HIST
    starts = range(base, base + CONV_K)
    return base, {r: [m for m in starts if m % SUBLANES == r] for r in range(SUBLANES)}


def _prompt_mix(zq, zk, zv, zsg, zlg, ubuf, xs, s_scr, first, on_ref, wdw_ref, bdw_ref, lnw_ref, lnb_ref, cmask_ref,
                mix_ref, ns_ref, nc_ref):
    tt = zq.shape[0]
    n_chunks = tt // GLA_CHUNK

    cmask = cmask_ref[...]
    qd, kd, kh, dec = _gla_decays(zq[...], zk[...], zlg[...], cmask, GLA_CHUNK)
    kh_t = kh.T
    dec_t = dec.T
    vb = zv[...].astype(BF16)
    causal = cmask > 0.5
    onorm = on_ref[...]
    for h in range(GLA_HEADS):
        dk = slice(h * GLA_DK, (h + 1) * GLA_DK)
        dv = slice(h * GLA_DV, (h + 1) * GLA_DV)
        qh = qd[:, dk].astype(BF16)
        a = lax.dot_general(qh, kd[:, dk].astype(BF16), (((1,), (1,)), ((), ())), preferred_element_type=F32)
        a = jnp.where(causal, a, 0.0).astype(BF16)
        vh = vb[:, dv]
        o_intra = jnp.dot(a, vh, preferred_element_type=F32)
        s = jnp.where(first, 0.0, s_scr[h])
        parts = []
        for c in range(n_chunks):
            r = slice(c * GLA_CHUNK, (c + 1) * GLA_CHUNK)
            parts.append(o_intra[r] + jnp.dot(qh[r], s.astype(BF16), preferred_element_type=F32))
            ktv = jnp.dot(kh_t[dk, r].astype(BF16), vh[r], preferred_element_type=F32)
            s = s * dec_t[dk, c * GLA_CHUNK:c * GLA_CHUNK + 1] + ktv
        s_scr[h] = s
        ns_ref[0, h] = s
        o = _head_rms(jnp.concatenate(parts, axis=0), onorm)
        mix_ref[:, dv] = (o * zsg[:, dv]).astype(BF16)

    cc = ubuf.shape[1]
    base, groups = _conv_tap_groups()
    for r, ms in groups.items():
        if r and ms:
            n = ms[-1] - ms[0] + tt
            xs[r - 1, 0:n, :] = ubuf[pl.ds(ms[0], n), :]
    bdw, lnw, lnb = bdw_ref[...], lnw_ref[...], lnb_ref[...]
    for blk in range(tt // CONV_ROWS):
        acc = None
        for r, ms in groups.items():
            for m in ms:
                if r:
                    win = xs[r - 1, pl.ds(m - ms[0] + blk * CONV_ROWS, CONV_ROWS), :]
                else:
                    win = ubuf[pl.ds(m + blk * CONV_ROWS, CONV_ROWS), :]
                term = win * wdw_ref[m - base:m - base + 1, :]
                acc = term if acc is None else acc + term
        oc = _conv_post(acc + bdw, lnw, lnb)
        mix_ref[blk * CONV_ROWS:(blk + 1) * CONV_ROWS, GLA_WIDTH:GLA_WIDTH + cc] = oc.astype(BF16)
    nc_ref[0] = ubuf[CONV_PAD + tt - CONV_HIST:CONV_PAD + tt, :]


def _ffn_in_mix_kernel(x_ref, n1_ref, wg_ref, wu_ref, wd_ref, nm_ref, win_ref, wa2_ref, ba_ref,
                       on_ref, wdw_ref, bdw_ref, lnw_ref, lnb_ref, cmask_ref,
                       x1_ref, mix_ref, ns_ref, nc_ref,
                       zq, zk, zv, zsg, zlg, ubuf, xs, s_scr, *, tiles_per_seq, n_tiles):
    j = pl.program_id(0)
    tt = x_ref.shape[0]

    @pl.when(j == 0)
    def _():
        for ref in (zq, zk, zv, zsg, zlg, ubuf, s_scr):
            ref[...] = jnp.zeros_like(ref)

    prev = jnp.maximum(j - 1, 0)
    _prompt_mix(zq, zk, zv, zsg, zlg, ubuf, xs, s_scr, prev % tiles_per_seq == 0,
                on_ref, wdw_ref, bdw_ref, lnw_ref, lnb_ref, cmask_ref, mix_ref, ns_ref, nc_ref)

    cur = jnp.minimum(j, n_tiles - 1)
    x1, q, k, v, sg, lg, u = _ffn_in_math(x_ref[...], n1_ref, wg_ref, wu_ref, wd_ref, nm_ref, win_ref, wa2_ref, ba_ref,
                                          ubuf.shape[1])
    x1_ref[...] = x1
    zq[...] = q
    zk[...] = k
    zv[...] = v
    zsg[...] = sg
    zlg[...] = lg
    ubuf[0:CONV_PAD, :] = jnp.where(cur % tiles_per_seq == 0, 0.0, ubuf[tt:tt + CONV_PAD, :])
    ubuf[CONV_PAD:CONV_PAD + tt, :] = u


def _ffn_in_mix(x, in_w, mix_w, batch, seq):
    m, d = x.shape
    tt = TOKEN_TILE
    assert seq % tt == 0 and tt % GLA_CHUNK == 0 and tt % CONV_ROWS == 0 and tt >= CONV_PAD >= CONV_HIST
    nt = seq // tt
    n = m // tt
    cc = mix_w[1].shape[1]
    consts = (*in_w, *mix_w, _chunk_masks(tt, GLA_CHUNK))
    cur = lambda j: (jnp.minimum(j, n - 1), 0)
    prev = lambda j: (jnp.maximum(j - 1, 0), 0)
    prev_seq3 = lambda j: (jnp.maximum(j - 1, 0) // nt, 0, 0)
    prev_seq4 = lambda j: (jnp.maximum(j - 1, 0) // nt, 0, 0, 0)
    _, groups = _conv_tap_groups()
    xs_rows = max(ms[-1] - ms[0] for r, ms in groups.items() if r and ms) + tt
    return pl.pallas_call(
        functools.partial(_ffn_in_mix_kernel, tiles_per_seq=nt, n_tiles=n),
        grid=(n + 1,),
        in_specs=[pl.BlockSpec((tt, d), cur)] + [_const_spec(a.shape) for a in consts],
        out_specs=[pl.BlockSpec((tt, d), cur),
                   pl.BlockSpec((tt, GLA_WIDTH + cc), prev),
                   pl.BlockSpec((1, GLA_HEADS, GLA_DK, GLA_DV), prev_seq4),
                   pl.BlockSpec((1, CONV_HIST, cc), prev_seq3)],
        out_shape=[jax.ShapeDtypeStruct((m, d), F32),
                   jax.ShapeDtypeStruct((m, GLA_WIDTH + cc), BF16),
                   jax.ShapeDtypeStruct((batch, GLA_HEADS, GLA_DK, GLA_DV), F32),
                   jax.ShapeDtypeStruct((batch, CONV_HIST, cc), F32)],
        scratch_shapes=[pltpu.VMEM((tt, GLA_QK), F32), pltpu.VMEM((tt, GLA_QK), F32), pltpu.VMEM((tt, GLA_WIDTH), F32),
                        pltpu.VMEM((tt, GLA_WIDTH), F32), pltpu.VMEM((tt, GLA_QK), F32),
                        pltpu.VMEM((CONV_PAD + tt, cc), F32), pltpu.VMEM((SUBLANES - 1, xs_rows, cc), F32),
                        pltpu.VMEM((GLA_HEADS, GLA_DK, GLA_DV), F32)],
        compiler_params=pltpu.CompilerParams(dimension_semantics=("arbitrary",), vmem_limit_bytes=VMEM_LIMIT_BYTES),
        name="ffn_in_mix",
    )(x, *consts)


def _ffn_in_kernel(x_ref, n1_ref, wg_ref, wu_ref, wd_ref, nm_ref, win_ref, wa2_ref, ba_ref,
                   x1_ref, q_ref, k_ref, v_ref, sg_ref, lg_ref, u_ref):
    outs = _ffn_in_math(x_ref[...], n1_ref, wg_ref, wu_ref, wd_ref, nm_ref, win_ref, wa2_ref, ba_ref, u_ref.shape[1])
    for ref, val in zip((x1_ref, q_ref, k_ref, v_ref, sg_ref, lg_ref, u_ref), outs):
        ref[...] = val


def _ffn_in(x, in_w, conv_ch):
    m, d = x.shape
    tm = TOKEN_TILE
    assert m % tm == 0
    row = lambda w: pl.BlockSpec((tm, w), lambda i: (i, 0))
    widths = (d, GLA_QK, GLA_QK, GLA_WIDTH, GLA_WIDTH, GLA_QK, conv_ch)
    return pl.pallas_call(
        _ffn_in_kernel,
        grid=(m // tm,),
        in_specs=[row(d)] + [_const_spec(a.shape) for a in in_w],
        out_specs=[row(w) for w in widths],
        out_shape=[jax.ShapeDtypeStruct((m, w), F32) for w in widths],
        compiler_params=pltpu.CompilerParams(dimension_semantics=("arbitrary",), vmem_limit_bytes=VMEM_LIMIT_BYTES),
        name="ffn_in",
    )(x, *in_w)


def _ffn_out_kernel(x1_ref, mix_ref, wout_ref, n2_ref, wg_ref, wu_ref, wd_ref, nf_ref, y_ref, *, final):
    x2 = x1_ref[...] + jnp.dot(mix_ref[...], wout_ref[...], preferred_element_type=F32)
    h = _rms(x2, n2_ref[...]).astype(BF16)
    x3 = x2 + 0.5 * _swiglu(h, wg_ref, wu_ref, wd_ref)
    y_ref[...] = _rms(x3, nf_ref[...]) if final else x3


def _ffn_out(x1, mix, out_w, nf, final):
    m, d = x1.shape
    tm = TOKEN_TILE
    assert m % tm == 0
    row = lambda w: pl.BlockSpec((tm, w), lambda i: (i, 0))
    return pl.pallas_call(
        functools.partial(_ffn_out_kernel, final=final),
        grid=(m // tm,),
        in_specs=[row(d), row(mix.shape[1])] + [_const_spec(a.shape) for a in (*out_w, nf)],
        out_specs=row(d),
        out_shape=jax.ShapeDtypeStruct((m, d), F32),
        compiler_params=pltpu.CompilerParams(dimension_semantics=("arbitrary",), vmem_limit_bytes=VMEM_LIMIT_BYTES),
        name="ffn_out",
    )(x1, mix, *out_w, nf)


def _mix_sample_kernel(q_ref, k_ref, v_ref, sg_ref, lg_ref, u_ref, s0_ref, c0_ref, on_ref, wdw_ref, bdw_ref,
                       lnw_ref, lnb_ref, cmask_ref, mix_ref, ns_ref, nc_ref, full, *, dec_seq):
    tt = q_ref.shape[0]
    bs = tt // dec_seq
    cmask = cmask_ref[...]
    qd, kd, kh, dec = _gla_decays(q_ref[...], k_ref[...], lg_ref[...], cmask, dec_seq)
    dec_t = dec.T
    v = v_ref[...]
    causal = cmask > 0.5
    onorm = on_ref[...]
    per_seq = lambda a: a.reshape(bs, dec_seq, a.shape[-1]).astype(BF16)
    for h in range(GLA_HEADS):
        dk = slice(h * GLA_DK, (h + 1) * GLA_DK)
        dv = slice(h * GLA_DV, (h + 1) * GLA_DV)
        qh = qd[:, dk]
        a = lax.dot_general(qh.astype(BF16), kd[:, dk].astype(BF16), (((1,), (1,)), ((), ())),
                            preferred_element_type=F32)
        a = jnp.where(causal, a, 0.0).astype(BF16)
        vh = v[:, dv]
        o = jnp.dot(a, vh.astype(BF16), preferred_element_type=F32)
        s0 = s0_ref[:, h]
        o = o + jnp.einsum('sqd,sdv->sqv', per_seq(qh), s0.astype(BF16),
                           preferred_element_type=F32).reshape(tt, GLA_DV)
        ktv = jnp.einsum('sqd,sqv->sdv', per_seq(kh[:, dk]), per_seq(vh), preferred_element_type=F32)
        for s in range(bs):
            ns_ref[s, h] = s0[s] * dec_t[dk, s * dec_seq:s * dec_seq + 1] + ktv[s]
        o = _head_rms(o, onorm)
        mix_ref[:, dv] = (o * sg_ref[:, dv]).astype(BF16)

    cc = u_ref.shape[1]
    full[:, 0:CONV_HIST, :] = c0_ref[...]
    full[:, CONV_HIST:CONV_HIST + dec_seq, :] = u_ref[...].reshape(bs, dec_seq, cc)
    acc = jnp.zeros((bs, dec_seq, cc), F32)
    for j in range(CONV_K):
        acc = acc + full[:, pl.ds(j, dec_seq), :] * wdw_ref[j:j + 1, :]
    oc = _conv_post(acc.reshape(tt, cc) + bdw_ref[...], lnw_ref[...], lnb_ref[...])
    mix_ref[:, GLA_WIDTH:GLA_WIDTH + cc] = oc.astype(BF16)
    nc_ref[...] = full[:, dec_seq:dec_seq + CONV_HIST, :]


def _mix_sample(q, k, v, sg, lg, u, s0, c0, mix_w, dec_batch, dec_seq):
    bs = SAMPLE_SEQS
    assert dec_batch % bs == 0 and dec_seq % SUBLANES == 0
    tt = bs * dec_seq
    cc = u.shape[1]
    row = lambda w: pl.BlockSpec((tt, w), lambda i: (i, 0))
    st_spec = pl.BlockSpec((bs, GLA_HEADS, GLA_DK, GLA_DV), lambda i: (i, 0, 0, 0))
    cv_spec = pl.BlockSpec((bs, CONV_HIST, cc), lambda i: (i, 0, 0))
    consts = (*mix_w, _chunk_masks(tt, dec_seq))
    full_rows = -(-(CONV_HIST + dec_seq) // SUBLANES) * SUBLANES
    return pl.pallas_call(
        functools.partial(_mix_sample_kernel, dec_seq=dec_seq),
        grid=(dec_batch // bs,),
        in_specs=[row(a.shape[1]) for a in (q, k, v, sg, lg, u)] + [st_spec, cv_spec]
        + [_const_spec(a.shape) for a in consts],
        out_specs=[row(GLA_WIDTH + cc), st_spec, cv_spec],
        out_shape=[jax.ShapeDtypeStruct((dec_batch * dec_seq, GLA_WIDTH + cc), BF16),
                   jax.ShapeDtypeStruct(s0.shape, F32),
                   jax.ShapeDtypeStruct(c0.shape, F32)],
        scratch_shapes=[pltpu.VMEM((bs, full_rows, cc), F32)],
        compiler_params=pltpu.CompilerParams(dimension_semantics=("arbitrary",), vmem_limit_bytes=VMEM_LIMIT_BYTES),
        name="mix_sample",
    )(q, k, v, sg, lg, u, s0, c0, *consts)


def kernel(x_prompt, x_sample, state_gla, state_conv, ffn1_norm, ffn1_wg, ffn1_wu, ffn1_wd, mix_norm, w_in, w_a2, b_a,
           gla_onorm, w_dw, b_dw, conv_ln_w, conv_ln_b, w_out, ffn2_norm, ffn2_wg, ffn2_wu, ffn2_wd, final_norm):
    depth = ffn1_norm.shape[0]
    batch, seq, d = x_prompt.shape
    dec_batch, dec_seq, _ = x_sample.shape
    cc = w_dw.shape[-1]
    rank = w_a2.shape[1]
    row = lambda a: a.reshape(1, -1)

    xp = x_prompt.reshape(batch * seq, d)
    xs = x_sample.reshape(dec_batch * dec_seq, d)
    gla_p, conv_p, gla_s, conv_s = [], [], [], []
    for l in range(depth):
        o_alr = 2 * GLA_QK + 2 * GLA_WIDTH
        win = jnp.concatenate([w_in[l][:, :o_alr], w_in[l][:, o_alr + rank:], w_in[l][:, o_alr:o_alr + rank]],
                              axis=1).astype(BF16)
        in_w = (row(ffn1_norm[l]), ffn1_wg[l].astype(BF16), ffn1_wu[l].astype(BF16), ffn1_wd[l].astype(BF16),
                row(mix_norm[l]), win, w_a2[l].astype(BF16), row(b_a[l]))
        mix_w = (row(gla_onorm[l]), w_dw[l], row(b_dw[l]), row(conv_ln_w[l]), row(conv_ln_b[l]))
        out_w = (w_out[l].astype(BF16), row(ffn2_norm[l]), ffn2_wg[l].astype(BF16), ffn2_wu[l].astype(BF16),
                 ffn2_wd[l].astype(BF16))
        last = l == depth - 1
        nf = row(final_norm)

        xp1, mixp, sp, bp = _ffn_in_mix(xp, in_w, mix_w, batch, seq)
        xs1, *zs = _ffn_in(xs, in_w, cc)
        mixs, ss, bs = _mix_sample(*zs, state_gla[l], state_conv[l], mix_w, dec_batch, dec_seq)
        xp = _ffn_out(xp1, mixp, out_w, nf, last)
        xs = _ffn_out(xs1, mixs, out_w, nf, last)
        gla_p.append(sp)
        conv_p.append(bp)
        gla_s.append(ss)
        conv_s.append(bs)
    return (xp.reshape(batch, seq, d), xs.reshape(dec_batch, dec_seq, d),
            jnp.stack(gla_p), jnp.stack(conv_p), jnp.stack(gla_s), jnp.stack(conv_s))
```
